```python
import math
import jax, jax.numpy as jnp
from jax import lax
import numpy as np

D_MODEL = 2048
BATCH = 2
SEQ = 8192
DEPTH = 2
DEC_BATCH = 4
DEC_SEQ = 4096
PAST_LEN = 128

HEAD_DIM = 128
N_ATTN_HEADS = 8
ATTN_WIDTH = N_ATTN_HEADS * HEAD_DIM
QK_DIM = HEAD_DIM // 2
N_GATE_GROUPS = 8
GATE_GROUP_DIM = 128
GATE_WIDTH = N_GATE_GROUPS * GATE_GROUP_DIM
CHUNK = 128
MIX_WIDTH = ATTN_WIDTH + GATE_WIDTH
IN_WIDTH = 3 * ATTN_WIDTH + 2 * GATE_WIDTH
Q_BLOCK = 128
ROPE_THETA = 10000.0
N_RET_HEADS = 8
KEY_DIM = 256
HALF_KEY = KEY_DIM // 2
N_KEYS = 128
N_EXPERTS = N_KEYS * N_KEYS
TOPK = 16
TOKEN_BLOCK = 128
EPS = 1e-6

kernel_name = "hymba_style_diffattn_gmlp_peer_encoder"


def rmsnorm(x, g):
    xf = x.astype(jnp.float32)
    y = xf * lax.rsqrt(jnp.mean(xf * xf, axis=-1, keepdims=True) + EPS)
    return (y * g.astype(jnp.float32)).astype(x.dtype)


def layernorm(x, g, b):
    xf = x.astype(jnp.float32)
    mu = jnp.mean(xf, axis=-1, keepdims=True)
    xc = xf - mu
    var = jnp.mean(xc * xc, axis=-1, keepdims=True)
    y = xc * lax.rsqrt(var + EPS) * g.astype(jnp.float32) + b.astype(jnp.float32)
    return y.astype(x.dtype)


def rope_tables(seq):
    pos = jnp.arange(seq, dtype=jnp.float32)
    inv = ROPE_THETA ** (-jnp.arange(0, QK_DIM, 2, dtype=jnp.float32) / QK_DIM)
    ang = pos[:, None] * inv[None, :]
    return jnp.cos(ang), jnp.sin(ang)


def apply_rope(x, cos, sin):
    shape = (1, cos.shape[0]) + (1,) * (x.ndim - 3) + (cos.shape[1],)
    c = cos.reshape(shape)
    s = sin.reshape(shape)
    xf = x.astype(jnp.float32)
    x1, x2 = xf[..., : QK_DIM // 2], xf[..., QK_DIM // 2:]
    out = jnp.concatenate([x1 * c - x2 * s, x2 * c + x1 * s], axis=-1)
    return out.astype(x.dtype)


def lambda_init_fn(layer):
    return 0.8 - 0.6 * math.exp(-0.3 * layer)


def diff_attention(q, k, v, lam):
    B, S = q.shape[0], q.shape[1]
    nb = S // Q_BLOCK
    scale = QK_DIM ** -0.5
    qb = q.reshape(B, nb, Q_BLOCK, N_ATTN_HEADS, 2, QK_DIM).transpose(1, 0, 2, 3, 4, 5)

    def block(qblk):
        s = jnp.einsum('bqhcd,bkhcd->bchqk', qblk, k).astype(jnp.float32) * scale
        p = jax.nn.softmax(s, axis=-1)
        a = p[:, 0] - lam * p[:, 1]
        return jnp.einsum('bhqk,bkhd->bqhd', a.astype(v.dtype), v)

    out = lax.map(block, qb)
    return out.transpose(1, 0, 2, 3, 4).reshape(B, S, N_ATTN_HEADS, HEAD_DIM)


def spatial_gating(gu, gv, ln_g, ln_b, sw, sb, out_g):
    B, S = gu.shape[0], gu.shape[1]
    vg = gv.reshape(B, S, N_GATE_GROUPS, GATE_GROUP_DIM)
    vg = layernorm(vg, ln_g.reshape(N_GATE_GROUPS, GATE_GROUP_DIM), ln_b.reshape(N_GATE_GROUPS, GATE_GROUP_DIM))
    vc = vg.reshape(B, S // CHUNK, CHUNK, N_GATE_GROUPS, GATE_GROUP_DIM)
    mixed = jnp.einsum('gpq,bnqgc->bnpgc', sw, vc) + sb.T[None, None, :, :, None]
    mixed = mixed.reshape(B, S, N_GATE_GROUPS, GATE_GROUP_DIM)
    out = gu.reshape(B, S, N_GATE_GROUPS, GATE_GROUP_DIM) * mixed
    out = rmsnorm(out, out_g.reshape(N_GATE_GROUPS, GATE_GROUP_DIM))
    return out.reshape(B, S, GATE_WIDTH)


def mixer_layer(x, cos, sin, layer, norm_mix, w_in, lq1, lk1, lq2, lk2, subln,
                gate_ln_g, gate_ln_b, spatial_w, spatial_b, gate_out_norm, w_out):
    B, S, _ = x.shape
    h = rmsnorm(x, norm_mix)
    z = h @ w_in
    q = z[..., :ATTN_WIDTH].reshape(B, S, N_ATTN_HEADS, 2, QK_DIM)
    k = z[..., ATTN_WIDTH:2 * ATTN_WIDTH].reshape(B, S, N_ATTN_HEADS, 2, QK_DIM)
    v = z[..., 2 * ATTN_WIDTH:3 * ATTN_WIDTH].reshape(B, S, N_ATTN_HEADS, HEAD_DIM)
    g = jax.nn.gelu(z[..., 3 * ATTN_WIDTH:])
    gu, gv = g[..., :GATE_WIDTH], g[..., GATE_WIDTH:]

    q = apply_rope(q, cos, sin)
    k = apply_rope(k, cos, sin)
    lam_init = lambda_init_fn(layer)
    lam = (jnp.exp(jnp.sum(lq1.astype(jnp.float32) * lk1.astype(jnp.float32)))
           - jnp.exp(jnp.sum(lq2.astype(jnp.float32) * lk2.astype(jnp.float32)))
           + lam_init)
    attn = diff_attention(q, k, v, lam)
    attn = (rmsnorm(attn, subln) * (1.0 - lam_init)).astype(x.dtype).reshape(B, S, ATTN_WIDTH)

    gate = spatial_gating(gu, gv, gate_ln_g, gate_ln_b, spatial_w, spatial_b, gate_out_norm)
    return x + jnp.concatenate([attn, gate], axis=-1) @ w_out


def peer_layer(x, norm_ffn, w_query, sub_keys, expert_down, expert_up):
    B, S, D = x.shape
    h = rmsnorm(x, norm_ffn)
    nb = (B * S) // TOKEN_BLOCK
    hb = h.reshape(nb, TOKEN_BLOCK, D)

    def block(hx):
        q = (hx @ w_query).reshape(TOKEN_BLOCK, N_RET_HEADS, 2, HALF_KEY)
        s = jnp.einsum('thcd,chkd->thck', q, sub_keys).astype(jnp.float32)
        sv, si = lax.top_k(s, TOPK)
        cand = (sv[:, :, 0, :, None] + sv[:, :, 1, None, :]).reshape(TOKEN_BLOCK, N_RET_HEADS, TOPK * TOPK)
        cidx = (si[:, :, 0, :, None] * N_KEYS + si[:, :, 1, None, :]).reshape(TOKEN_BLOCK, N_RET_HEADS, TOPK * TOPK)
        fv, fi = lax.top_k(cand, TOPK)
        eidx = jnp.take_along_axis(cidx, fi, axis=-1)
        gates = jax.nn.softmax(fv, axis=-1)
        u = jnp.take(expert_down, eidx, axis=0)
        act = jax.nn.gelu(jnp.einsum('thkd,td->thk', u, hx))
        w = (gates * act.astype(jnp.float32)).astype(hx.dtype)
        vv = jnp.take(expert_up, eidx, axis=0)
        return jnp.einsum('thk,thkd->td', w, vv)

    out = lax.map(block, hb).reshape(B, S, D)
    return x + out


def trunk(x, norm_mix, w_in, lambda_q1, lambda_k1, lambda_q2, lambda_k2, subln,
          gate_ln_g, gate_ln_b, spatial_w, spatial_b, gate_out_norm, w_out,
          norm_ffn, w_query, sub_keys, expert_down, expert_up, norm_final):
    cos, sin = rope_tables(x.shape[1])
    for l in range(DEPTH):
        x = mixer_layer(x, cos, sin, l, norm_mix[l], w_in[l], lambda_q1[l], lambda_k1[l],
                        lambda_q2[l], lambda_k2[l], subln[l], gate_ln_g[l], gate_ln_b[l],
                        spatial_w[l], spatial_b[l], gate_out_norm[l], w_out[l])
        x = peer_layer(x, norm_ffn[l], w_query[l], sub_keys[l], expert_down[l], expert_up[l])
    return rmsnorm(x, norm_final)


def setup_inputs(seed: int = 0) -> dict:
    key = jax.random.key(seed)
    ks = jax.random.split(key, 24)
    f32 = jnp.float32
    nrm = lambda k, shape, std: jax.random.normal(k, shape, f32) * std
    return {
        "x_prompt": nrm(ks[0], (BATCH, SEQ, D_MODEL), 1.0),
        "x_sample": nrm(ks[1], (DEC_BATCH, DEC_SEQ, D_MODEL), 1.0),
        "norm_mix": 1.0 + nrm(ks[2], (DEPTH, D_MODEL), 0.02),
        "w_in": nrm(ks[3], (DEPTH, D_MODEL, IN_WIDTH), D_MODEL ** -0.5),
        "lambda_q1": nrm(ks[4], (DEPTH, QK_DIM), 0.1),
        "lambda_k1": nrm(ks[5], (DEPTH, QK_DIM), 0.1),
        "lambda_q2": nrm(ks[6], (DEPTH, QK_DIM), 0.1),
        "lambda_k2": nrm(ks[7], (DEPTH, QK_DIM), 0.1),
        "subln": 1.0 + nrm(ks[8], (DEPTH, HEAD_DIM), 0.02),
        "gate_ln_g": 1.0 + nrm(ks[9], (DEPTH, GATE_WIDTH), 0.02),
        "gate_ln_b": nrm(ks[10], (DEPTH, GATE_WIDTH), 0.02),
        "spatial_w": nrm(ks[11], (DEPTH, N_GATE_GROUPS, CHUNK, CHUNK), CHUNK ** -0.5),
        "spatial_b": 1.0 + nrm(ks[12], (DEPTH, N_GATE_GROUPS, CHUNK), 0.02),
        "gate_out_norm": 1.0 + nrm(ks[13], (DEPTH, GATE_WIDTH), 0.02),
        "w_out": nrm(ks[14], (DEPTH, MIX_WIDTH, D_MODEL), MIX_WIDTH ** -0.5),
        "norm_ffn": 1.0 + nrm(ks[15], (DEPTH, D_MODEL), 0.02),
        "w_query": nrm(ks[16], (DEPTH, D_MODEL, N_RET_HEADS * KEY_DIM), D_MODEL ** -0.5),
        "sub_keys": nrm(ks[17], (DEPTH, 2, N_RET_HEADS, N_KEYS, HALF_KEY), HALF_KEY ** -0.5),
        "expert_down": nrm(ks[18], (DEPTH, N_EXPERTS, D_MODEL), D_MODEL ** -0.5),
        "expert_up": nrm(ks[19], (DEPTH, N_EXPERTS, D_MODEL), (N_RET_HEADS * TOPK) ** -0.5),
        "norm_final": 1.0 + nrm(ks[20], (D_MODEL,), 0.02),
    }


def reference(x_prompt, x_sample, norm_mix, w_in, lambda_q1, lambda_k1, lambda_q2, lambda_k2,
              subln, gate_ln_g, gate_ln_b, spatial_w, spatial_b, gate_out_norm, w_out,
              norm_ffn, w_query, sub_keys, expert_down, expert_up, norm_final):
    y_prompt = trunk(x_prompt, norm_mix, w_in, lambda_q1, lambda_k1, lambda_q2, lambda_k2, subln,
                     gate_ln_g, gate_ln_b, spatial_w, spatial_b, gate_out_norm, w_out,
                     norm_ffn, w_query, sub_keys, expert_down, expert_up, norm_final)
    y_sample = trunk(x_sample, norm_mix, w_in, lambda_q1, lambda_k1, lambda_q2, lambda_k2, subln,
                     gate_ln_g, gate_ln_b, spatial_w, spatial_b, gate_out_norm, w_out,
                     norm_ffn, w_query, sub_keys, expert_down, expert_up, norm_final)
    return (y_prompt, y_sample)
```

```python
import functools
import math

import jax
import jax.numpy as jnp
from jax import lax
from jax.experimental import pallas as pl
from jax.experimental.pallas import tpu as pltpu

F32 = jnp.float32
BF16 = jnp.bfloat16

EPS = 1e-6
HEAD_DIM = 128
QK_DIM = HEAD_DIM // 2
N_ATTN_HEADS = 8
N_GATE_GROUPS = 8
ATTN_WIDTH = N_ATTN_HEADS * HEAD_DIM
GATE_WIDTH = N_GATE_GROUPS * HEAD_DIM
CHUNK = 128
ROPE_THETA = 10000.0
N_RET_HEADS = 8
N_KEYS = 128
HALF_KEY = 128
TOPK = 16

V7X_VMEM_LIMIT_BYTES = 56 * 1024 * 1024


def _tiles(n_tokens):
    def pick(pref):
        t = pref
        while n_tokens % t:
            t //= 2
        return t
    return dict(
        in_tm=pick(512),
        attn_tq=256, attn_tk=512,
        gate_tb=pick(512),
        out_tm=pick(512),
        router_tb=pick(256),
        exp_tb=pick(512), exp_eb=1024,
    )


def _cparams(sem):
    return pltpu.CompilerParams(dimension_semantics=sem, vmem_limit_bytes=V7X_VMEM_LIMIT_BYTES)


def _rms(x):
    return x * lax.rsqrt(jnp.mean(x * x, axis=-1, keepdims=True) + EPS)


def _in_proj_kernel(x_ref, g_ref, w_ref, cos_ref, sin_ref,
                    q_ref, k_ref, v_ref, gu_ref, gv_ref, h_scr, *, q_scale):
    j = pl.program_id(1)

    @pl.when(j == 0)
    def _():
        h_scr[...] = (_rms(x_ref[...]) * g_ref[...]).astype(BF16)

    z = jnp.dot(h_scr[...], w_ref[...], preferred_element_type=F32)

    def rope_store(o_ref, mul):
        c = cos_ref[...]
        s = sin_ref[...]
        lane = lax.broadcasted_iota(jnp.int32, c.shape, 1)
        first = (lane % QK_DIM) < (QK_DIM // 2)
        for h in range(N_ATTN_HEADS):
            zh = z[:, h * HEAD_DIM:(h + 1) * HEAD_DIM]
            partner = jnp.where(first,
                                pltpu.roll(zh, HEAD_DIM - QK_DIM // 2, 1),
                                pltpu.roll(zh, QK_DIM // 2, 1))
            o = zh * c + partner * s
            o_ref[:, h * HEAD_DIM:(h + 1) * HEAD_DIM] = (o * mul).astype(BF16)

    @pl.when(j == 0)
    def _():
        rope_store(q_ref, q_scale)

    @pl.when(j == 1)
    def _():
        rope_store(k_ref, 1.0)

    @pl.when(j == 2)
    def _():
        v_ref[...] = z.astype(BF16)

    @pl.when(j == 3)
    def _():
        gu_ref[...] = jax.nn.gelu(z)

    @pl.when(j == 4)
    def _():
        gv_ref[...] = jax.nn.gelu(z)


def _in_proj(x, g, w_bf16, cos_t, sin_t, tm):
    m, d = x.shape
    tn = ATTN_WIDTH
    assert w_bf16.shape == (d, 3 * ATTN_WIDTH + 2 * GATE_WIDTH) and ATTN_WIDTH == GATE_WIDTH
    row = lambda i, j: (i, 0)
    out_bf = jax.ShapeDtypeStruct((m, tn), BF16)
    out_f = jax.ShapeDtypeStruct((m, tn), F32)
    return pl.pallas_call(
        functools.partial(_in_proj_kernel, q_scale=QK_DIM ** -0.5),
        grid=(m // tm, 5),
        in_specs=[
            pl.BlockSpec((tm, d), row),
            pl.BlockSpec((1, d), lambda i, j: (0, 0)),
            pl.BlockSpec((d, tn), lambda i, j: (0, j)),
            pl.BlockSpec((tm, HEAD_DIM), row),
            pl.BlockSpec((tm, HEAD_DIM), row),
        ],
        out_specs=[pl.BlockSpec((tm, tn), row)] * 5,
        out_shape=[out_bf, out_bf, out_bf, out_f, out_f],
        scratch_shapes=[pltpu.VMEM((tm, d), BF16)],
        compiler_params=_cparams(("parallel", "arbitrary")),
        name="in_proj",
    )(x, g.reshape(1, d), w_bf16, cos_t, sin_t)


def _attn_kernel(lam_ref, q_ref, k_ref, v_ref, g_ref, o_ref, *, tk, n_kv, post_scale):
    q = q_ref[...]
    lane = lax.broadcasted_iota(jnp.int32, q.shape, 1)
    zero = jnp.zeros_like(q)
    q_parts = (jnp.where(lane < QK_DIM, q, zero), jnp.where(lane >= QK_DIM, q, zero))
    tq = q.shape[0]

    def body(i, carry):
        start = pl.multiple_of(i * tk, tk)
        kc = k_ref[pl.ds(start, tk), :]
        vc = v_ref[pl.ds(start, tk), :]
        new = []
        for c in range(2):
            m, l, acc = carry[c]
            s = lax.dot_general(q_parts[c], kc, (((1,), (1,)), ((), ())),
                                preferred_element_type=F32)
            m_new = jnp.maximum(m, jnp.max(s, axis=-1, keepdims=True))
            p = jnp.exp(s - m_new)
            alpha = jnp.exp(m - m_new)
            l = alpha * l + jnp.sum(p, axis=-1, keepdims=True)
            acc = alpha * acc + jnp.dot(p.astype(BF16), vc, preferred_element_type=F32)
            new.append((m_new, l, acc))
        return tuple(new)

    init = tuple((jnp.full((tq, 1), -jnp.inf, F32), jnp.zeros((tq, 1), F32),
                  jnp.zeros((tq, HEAD_DIM), F32)) for _ in range(2))
    (_, l1, a1), (_, l2, a2) = lax.fori_loop(0, n_kv, body, init)
    o = a1 / l1 - lam_ref[0] * (a2 / l2)
    o_ref[...] = (_rms(o) * g_ref[...] * post_scale).astype(BF16)


def _attention(q, k, v, lam, subln, post_scale, row_off, batch, seq, tq, tk):
    assert row_off % seq == 0 and seq % tq == 0 and seq % tk == 0
    nq = seq // tq
    qoff = row_off // tq
    soff = row_off // seq
    return pl.pallas_call(
        functools.partial(_attn_kernel, tk=tk, n_kv=seq // tk, post_scale=post_scale),
        grid_spec=pltpu.PrefetchScalarGridSpec(
            num_scalar_prefetch=1,
            grid=(batch, N_ATTN_HEADS, nq),
            in_specs=[
                pl.BlockSpec((tq, HEAD_DIM), lambda b, h, i, lam: (qoff + b * nq + i, h)),
                pl.BlockSpec((seq, HEAD_DIM), lambda b, h, i, lam: (soff + b, h)),
                pl.BlockSpec((seq, HEAD_DIM), lambda b, h, i, lam: (soff + b, h)),
                pl.BlockSpec((1, HEAD_DIM), lambda b, h, i, lam: (0, 0)),
            ],
            out_specs=pl.BlockSpec((tq, HEAD_DIM), lambda b, h, i, lam: (b * nq + i, h)),
        ),
        out_shape=jax.ShapeDtypeStruct((batch * seq, ATTN_WIDTH), BF16),
        compiler_params=_cparams(("parallel", "parallel", "arbitrary")),
        name="diff_attention",
    )(lam.reshape(1), q, k, v, subln.reshape(1, HEAD_DIM))


def _gate_kernel(gu_ref, gv_ref, lng_ref, lnb_ref, sw_ref, sb_ref, og_ref, o_ref, *, n_chunks):
    for g in range(N_GATE_GROUPS):
        cs = slice(g * HEAD_DIM, (g + 1) * HEAD_DIM)
        lng = lng_ref[:, cs]
        lnb = lnb_ref[:, cs]
        og = og_ref[:, cs]
        sw = sw_ref[g]
        sb = sb_ref[g]
        for c in range(n_chunks):
            rs = slice(c * CHUNK, (c + 1) * CHUNK)
            v = gv_ref[rs, cs]
            xc = v - jnp.mean(v, axis=-1, keepdims=True)
            var = jnp.mean(xc * xc, axis=-1, keepdims=True)
            y = xc * lax.rsqrt(var + EPS) * lng + lnb
            mixed = jnp.dot(sw, y.astype(BF16), preferred_element_type=F32) + sb
            o = gu_ref[rs, cs] * mixed
            o_ref[rs, cs] = (_rms(o) * og).astype(BF16)


def _gating(gu, gv, ln_g, ln_b, sw_bf16, sb, out_g, tb):
    m = gu.shape[0]
    assert tb % CHUNK == 0
    sb_b = jnp.broadcast_to(sb[:, :, None], (N_GATE_GROUPS, CHUNK, HEAD_DIM))
    row = lambda i: (i, 0)
    vec = pl.BlockSpec((1, GATE_WIDTH), lambda i: (0, 0))
    cube = pl.BlockSpec((N_GATE_GROUPS, CHUNK, HEAD_DIM), lambda i: (0, 0, 0))
    return pl.pallas_call(
        functools.partial(_gate_kernel, n_chunks=tb // CHUNK),
        grid=(m // tb,),
        in_specs=[pl.BlockSpec((tb, GATE_WIDTH), row), pl.BlockSpec((tb, GATE_WIDTH), row),
                  vec, vec, cube, cube, vec],
        out_specs=pl.BlockSpec((tb, GATE_WIDTH), row),
        out_shape=jax.ShapeDtypeStruct((m, GATE_WIDTH), BF16),
        compiler_params=_cparams(("parallel",)),
        name="spatial_gating",
    )(gu, gv, ln_g.reshape(1, -1), ln_b.reshape(1, -1), sw_bf16, sb_b, out_g.reshape(1, -1))


def _out_proj_kernel(x_ref, a_ref, gt_ref, w_ref, nf_ref, x1_ref, h2_ref):
    y = (x_ref[...]
         + jnp.dot(a_ref[...], w_ref[:ATTN_WIDTH, :], preferred_element_type=F32)
         + jnp.dot(gt_ref[...], w_ref[ATTN_WIDTH:, :], preferred_element_type=F32))
    x1_ref[...] = y
    h2_ref[...] = (_rms(y) * nf_ref[...]).astype(BF16)


def _out_proj(x, attn, gate, w_bf16, norm_ffn, tm):
    m, d = x.shape
    row = lambda i: (i, 0)
    return pl.pallas_call(
        _out_proj_kernel,
        grid=(m // tm,),
        in_specs=[pl.BlockSpec((tm, d), row),
                  pl.BlockSpec((tm, ATTN_WIDTH), row),
                  pl.BlockSpec((tm, GATE_WIDTH), row),
                  pl.BlockSpec((ATTN_WIDTH + GATE_WIDTH, d), lambda i: (0, 0)),
                  pl.BlockSpec((1, d), lambda i: (0, 0))],
        out_specs=[pl.BlockSpec((tm, d), row), pl.BlockSpec((tm, d), row)],
        out_shape=[jax.ShapeDtypeStruct((m, d), F32), jax.ShapeDtypeStruct((m, d), BF16)],
        compiler_params=_cparams(("parallel",)),
        name="out_proj",
    )(x, attn, gate, w_bf16, norm_ffn.reshape(1, d))


_CAND_PAIRS = [(a, b) for a in range(TOPK) for b in range(TOPK) if (a + 1) * (b + 1) <= TOPK]
_CAND_ROWS = -(-len(_CAND_PAIRS) // 8) * 8


def _extract_top(x, n):
    rows = []
    for _ in range(n):
        m = jnp.max(x, axis=0, keepdims=True)
        rows.append(m)
        x = jnp.where(x == m, -jnp.inf, x)
    return rows


def _router_kernel(h_ref, wq_ref, keys_ref, u1_ref, u2_ref, tau_ref, cand_scr):
    q = jnp.dot(h_ref[...], wq_ref[...], preferred_element_type=F32).astype(BF16)
    cand_scr[...] = jnp.full(cand_scr.shape, -jnp.inf, F32)

    def best_pair_sums(rows1, rows2):
        for r, (a, b) in enumerate(_CAND_PAIRS):
            cand_scr[r:r + 1, :] = rows1[a] + rows2[b]
        return _extract_top(cand_scr[...], TOPK)

    for h in range(N_RET_HEADS):
        shifted, tops = [], []
        for c in range(2):
            col = (h * 2 + c) * HALF_KEY
            s_t = lax.dot_general(keys_ref[c, h], q[:, col:col + HALF_KEY],
                                  (((1,), (1,)), ((), ())), preferred_element_type=F32)
            top = _extract_top(s_t, TOPK)
            shifted.append(s_t - top[0])
            tops.append([row - top[0] for row in top])
        best = best_pair_sums(tops[0], tops[1])
        z = jnp.exp(best[0])
        for r in range(1, TOPK):
            z = z + jnp.exp(best[r])
        log_z = jnp.log(z)
        u1_ref[h] = shifted[0]
        u2_ref[h] = shifted[1] - log_z
        best = best_pair_sums(tops[0], [row - log_z for row in tops[1]])
        tau_ref[h:h + 1, :] = best[TOPK - 1]


def _router(h2, wq_bf16, keys_bf16, tb):
    m, d = h2.shape
    qw = wq_bf16.shape[1]
    assert qw == N_RET_HEADS * 2 * HALF_KEY
    stat = jax.ShapeDtypeStruct((N_RET_HEADS, m), F32)
    score = jax.ShapeDtypeStruct((N_RET_HEADS, N_KEYS, m), F32)
    return pl.pallas_call(
        _router_kernel,
        grid=(m // tb,),
        in_specs=[pl.BlockSpec((tb, d), lambda i: (i, 0)),
                  pl.BlockSpec((d, qw), lambda i: (0, 0)),
                  pl.BlockSpec((2, N_RET_HEADS, N_KEYS, HALF_KEY), lambda i: (0, 0, 0, 0))],
        out_specs=[pl.BlockSpec((N_RET_HEADS, N_KEYS, tb), lambda i: (0, 0, i)),
                   pl.BlockSpec((N_RET_HEADS, N_KEYS, tb), lambda i: (0, 0, i)),
                   pl.BlockSpec((N_RET_HEADS, tb), lambda i: (0, i))],
        out_shape=[score, score, stat],
        scratch_shapes=[pltpu.VMEM((_CAND_ROWS, tb), F32)],
        compiler_params=_cparams(("parallel",)),
        name="peer_router",
    )(h2, wq_bf16, keys_bf16)


def _experts_kernel(x_ref, h_ref, ed_ref, eut_ref, u1_ref, u2_ref, tau_ref, *rest,
                    n_eblocks, final_norm):
    if final_norm:
        gf_ref, o_ref, acc_scr, w_scr = rest
    else:
        o_ref, acc_scr, w_scr = rest
    j = pl.program_id(1)
    eb, tb = w_scr.shape
    rows_per_block = eb // N_KEYS

    @pl.when(j == 0)
    def _():
        acc_scr[...] = jnp.zeros(acc_scr.shape, F32)

    a_t = lax.dot_general(ed_ref[...], h_ref[...], (((1,), (1,)), ((), ())),
                          preferred_element_type=F32)
    act = jax.nn.gelu(a_t)

    for ii in range(rows_per_block):
        for lc in range(tb // 128):
            ls = slice(lc * 128, (lc + 1) * 128)
            gate = jnp.zeros((N_KEYS, 128), F32)
            for h in range(N_RET_HEADS):
                pair = u2_ref[h, :, ls] + u1_ref[h, ii:ii + 1, ls]
                gate = gate + jnp.where(pair >= tau_ref[h:h + 1, ls], jnp.exp(pair), 0.0)
            w_scr[ii * N_KEYS:(ii + 1) * N_KEYS, ls] = (
                gate * act[ii * N_KEYS:(ii + 1) * N_KEYS, ls]).astype(BF16)

    acc_scr[...] += jnp.dot(eut_ref[...], w_scr[...], preferred_element_type=F32)

    @pl.when(j == n_eblocks - 1)
    def _():
        y = x_ref[...] + acc_scr[...].T
        if final_norm:
            y = _rms(y) * gf_ref[...]
        o_ref[...] = y


def _experts(x1, h2, ed_bf16, eut_bf16, u1, u2, tau, norm_final, tb, eb):
    m, d = x1.shape
    n_exp = ed_bf16.shape[0]
    assert n_exp == N_KEYS * N_KEYS and eb % (8 * N_KEYS) == 0 and n_exp % eb == 0 and tb % 128 == 0
    n_eblocks = n_exp // eb
    final_norm = norm_final is not None
    tok = lambda i, j: (i, 0)
    in_specs = [
        pl.BlockSpec((tb, d), tok),
        pl.BlockSpec((tb, d), tok),
        pl.BlockSpec((eb, d), lambda i, j: (j, 0)),
        pl.BlockSpec((d, eb), lambda i, j: (0, j)),
        pl.BlockSpec((N_RET_HEADS, eb // N_KEYS, tb), lambda i, j: (0, j, i)),
        pl.BlockSpec((N_RET_HEADS, N_KEYS, tb), lambda i, j: (0, 0, i)),
        pl.BlockSpec((N_RET_HEADS, tb), lambda i, j: (0, i)),
    ]
    args = [x1, h2, ed_bf16, eut_bf16, u1, u2, tau]
    if final_norm:
        in_specs.append(pl.BlockSpec((1, d), lambda i, j: (0, 0)))
        args.append(norm_final.reshape(1, d))
    return pl.pallas_call(
        functools.partial(_experts_kernel, n_eblocks=n_eblocks, final_norm=final_norm),
        grid=(m // tb, n_eblocks),
        in_specs=in_specs,
        out_specs=pl.BlockSpec((tb, d), tok),
        out_shape=jax.ShapeDtypeStruct((m, d), F32),
        scratch_shapes=[pltpu.VMEM((d, tb), F32), pltpu.VMEM((eb, tb), BF16)],
        compiler_params=_cparams(("parallel", "arbitrary")),
        name="peer_experts",
    )(*args)


def _rope_tables(seq_lens):
    pos = jnp.concatenate([jnp.tile(jnp.arange(s, dtype=F32), b) for b, s in seq_lens])
    inv = ROPE_THETA ** (-jnp.arange(0, QK_DIM, 2, dtype=F32) / QK_DIM)
    ang = pos[:, None] * inv[None, :]
    reps = HEAD_DIM // (QK_DIM // 2)
    cos_t = jnp.tile(jnp.cos(ang), (1, reps))
    sign = jnp.where((jnp.arange(HEAD_DIM) % QK_DIM) < QK_DIM // 2, -1.0, 1.0).astype(F32)
    sin_t = jnp.tile(jnp.sin(ang), (1, reps)) * sign[None, :]
    return cos_t, sin_t


def _lambda_init(layer):
    return 0.8 - 0.6 * math.exp(-0.3 * layer)


def kernel(x_prompt, x_sample, norm_mix, w_in, lambda_q1, lambda_k1, lambda_q2, lambda_k2, subln,
           gate_ln_g, gate_ln_b, spatial_w, spatial_b, gate_out_norm, w_out, norm_ffn, w_query,
           sub_keys, expert_down, expert_up, norm_final):
    d = x_prompt.shape[-1]
    groups = [(x_prompt.shape[0], x_prompt.shape[1]), (x_sample.shape[0], x_sample.shape[1])]
    x = jnp.concatenate([x_prompt.reshape(-1, d), x_sample.reshape(-1, d)], axis=0)
    n_tokens = x.shape[0]
    t = _tiles(n_tokens)
    cos_t, sin_t = _rope_tables(groups)
    depth = w_in.shape[0]

    for l in range(depth):
        lam_init = _lambda_init(l)
        lam = (jnp.exp(jnp.sum(lambda_q1[l] * lambda_k1[l]))
               - jnp.exp(jnp.sum(lambda_q2[l] * lambda_k2[l])) + lam_init).astype(F32)

        q, k, v, gu, gv = _in_proj(x, norm_mix[l], w_in[l].astype(BF16), cos_t, sin_t, t["in_tm"])

        attn_parts = []
        row_off = 0
        for b, s in groups:
            attn_parts.append(_attention(q, k, v, lam, subln[l], 1.0 - lam_init, row_off, b, s,
                                         min(t["attn_tq"], s), min(t["attn_tk"], s)))
            row_off += b * s
        attn = jnp.concatenate(attn_parts, axis=0)

        gate = _gating(gu, gv, gate_ln_g[l], gate_ln_b[l], spatial_w[l].astype(BF16),
                       spatial_b[l], gate_out_norm[l], t["gate_tb"])

        x1, h2 = _out_proj(x, attn, gate, w_out[l].astype(BF16), norm_ffn[l], t["out_tm"])

        u1, u2, tau = _router(h2, w_query[l].astype(BF16), sub_keys[l].astype(BF16),
                              t["router_tb"])

        x = _experts(x1, h2, expert_down[l].astype(BF16), expert_up[l].astype(BF16).T,
                     u1, u2, tau, norm_final if l == depth - 1 else None,
                     t["exp_tb"], t["exp_eb"])

    n_prompt = groups[0][0] * groups[0][1]
    return (x[:n_prompt].reshape(x_prompt.shape), x[n_prompt:].reshape(x_sample.shape))
```
